```python
import jax, jax.numpy as jnp
from jax import lax
import numpy as np

D_MODEL = 1024
BATCH = 4
SEQ = 4096
DEPTH = 2
DEC_BATCH = 32
DEC_SEQ = 1
PAST_LEN = 16384
PAGE_SIZE = 128

HEAD_DIM = 64
H_SB = 8
H_FOX = 8
H_ATT = H_SB + H_FOX
ATT_W = H_ATT * HEAD_DIM
Q_BLOCK = 128
CONV_W = 512
CONV_K = 31
POOL_W = 512
POOL_WINDOWS = (2, 4, 8, 16)
POOL_GROUPS = 4
POOL_GROUP_W = POOL_W // POOL_GROUPS
POOL_CTX = max(POOL_WINDOWS) - 1
N_EXPERTS = 64
TOP_K = 8
N_EXPERT_GROUPS = 8
TOPK_GROUPS = 4
EXPERT_FF = 256
SHARED_FF = 256
ROUTED_SCALE = 2.5
MOE_BLOCK = 128
PLE_DIM = 256
N_ATT_LAYERS = (DEPTH + 1) // 2
N_CONV_LAYERS = DEPTH // 2
ALPHA = (2 * DEPTH) ** 0.25
BETA = (8 * DEPTH) ** -0.25
LN_EPS = 1e-5

kernel_name = 'hybrid_stickbreak_fox_conformer_pool_moe_step'


def layer_norm(x, g, b):
    xf = x.astype(jnp.float32)
    mu = xf.mean(-1, keepdims=True)
    var = jnp.mean(jnp.square(xf - mu), -1, keepdims=True)
    return ((xf - mu) * lax.rsqrt(var + LN_EPS) * g + b).astype(x.dtype)


def post_norm(x, f, g, b):
    return layer_norm(ALPHA * x + f, g, b)


def attn_weights(z, fq, fk, q_pos, k_pos):
    strict = k_pos[None, :] < q_pos[:, None]
    incl = k_pos[None, :] <= q_pos[:, None]
    z_sb, z_fx = z[:, :H_SB], z[:, H_SB:]
    log_keep = jnp.where(strict, jax.nn.log_sigmoid(-z_sb), 0.0)
    between = lax.cumsum(log_keep, axis=3, reverse=True) - log_keep
    w_sb = jnp.where(strict, jnp.exp(jax.nn.log_sigmoid(z_sb) + between), 0.0)
    bias = jnp.swapaxes(fq, 1, 2)[:, :, :, None] - jnp.swapaxes(fk, 1, 2)[:, :, None, :]
    w_fx = jax.nn.softmax(jnp.where(incl, z_fx + bias, -jnp.inf), axis=-1)
    return jnp.concatenate([w_sb, w_fx], axis=1)


def att_project(h, w_in, b_f):
    b, t, _ = h.shape
    proj = h @ w_in
    q = proj[..., :ATT_W].reshape(b, t, H_ATT, HEAD_DIM)
    k = proj[..., ATT_W:2 * ATT_W].reshape(b, t, H_ATT, HEAD_DIM)
    v = proj[..., 2 * ATT_W:3 * ATT_W].reshape(b, t, H_ATT, HEAD_DIM)
    logf = jax.nn.log_sigmoid((proj[..., 3 * ATT_W:] + b_f).astype(jnp.float32))
    return q, k, v, logf


def attention_prompt(h, w_in, b_f, w_out):
    b, t, _ = h.shape
    q, k, v, logf = att_project(h, w_in, b_f)
    fcum = jnp.cumsum(logf, axis=1)
    nb = t // Q_BLOCK
    k_pos = jnp.arange(t)
    scale = HEAD_DIM ** -0.5

    def block(args):
        q_blk, fq_blk, i = args
        q_pos = i * Q_BLOCK + jnp.arange(Q_BLOCK)
        z = jnp.einsum('bqhd,bkhd->bhqk', q_blk, k).astype(jnp.float32) * scale
        w = attn_weights(z, fq_blk, fcum, q_pos, k_pos)
        return jnp.einsum('bhqk,bkhd->bqhd', w.astype(v.dtype), v)

    q_blocks = jnp.swapaxes(q.reshape(b, nb, Q_BLOCK, H_ATT, HEAD_DIM), 0, 1)
    fq_blocks = jnp.swapaxes(fcum.reshape(b, nb, Q_BLOCK, H_FOX), 0, 1)
    o = lax.map(block, (q_blocks, fq_blocks, jnp.arange(nb)))
    o = jnp.swapaxes(o, 0, 1).reshape(b, t, ATT_W)
    return o @ w_out, k, v, logf


def attention_sample(h, cache_k, cache_v, cache_logf, layer, page_table, w_in, b_f, w_out):
    b, t, _ = h.shape
    q, k, v, logf = att_project(h, w_in, b_f)
    past = page_table.shape[1] * cache_k.shape[2]
    k_past = cache_k[layer, page_table].reshape(b, past, H_ATT, HEAD_DIM)
    v_past = cache_v[layer, page_table].reshape(b, past, H_ATT, HEAD_DIM)
    lf_past = cache_logf[layer, page_table].reshape(b, past, H_FOX).astype(jnp.float32)
    fcum = jnp.cumsum(jnp.concatenate([lf_past, logf], axis=1), axis=1)
    q_pos = past + jnp.arange(t)
    k_pos = jnp.arange(past + t)
    scale = HEAD_DIM ** -0.5
    z = jnp.concatenate([jnp.einsum('bqhd,bkhd->bhqk', q, k_past),
                         jnp.einsum('bqhd,bkhd->bhqk', q, k)], axis=-1).astype(jnp.float32) * scale
    w = attn_weights(z, fcum[:, past:], fcum, q_pos, k_pos).astype(v.dtype)
    o = (jnp.einsum('bhqk,bkhd->bqhd', w[..., :past], v_past)
         + jnp.einsum('bhqk,bkhd->bqhd', w[..., past:], v))
    return o.reshape(b, t, ATT_W) @ w_out, k, v, logf


def multiscale_pool(p_in, pos):
    t = pos.shape[0]
    cs = jnp.cumsum(p_in.astype(jnp.float32), axis=1)
    cs = jnp.concatenate([jnp.zeros_like(cs[:, :1]), cs], axis=1)
    end = cs[:, POOL_CTX + 1:]
    outs = []
    for g, w in enumerate(POOL_WINDOWS):
        sl = slice(g * POOL_GROUP_W, (g + 1) * POOL_GROUP_W)
        start = cs[:, POOL_CTX + 1 - w:POOL_CTX + 1 - w + t, sl]
        count = jnp.minimum(w, pos + 1).astype(jnp.float32)[None, :, None]
        outs.append((end[..., sl] - start) / count)
    mean = jnp.concatenate(outs, axis=-1)
    return (mean - p_in[:, POOL_CTX:].astype(jnp.float32)).astype(p_in.dtype)


def conv_pool_mixer(h, conv_ctx, pool_ctx, pos, w_in, w_dw, b_dw, ln_g, ln_b, w_pool, pool_scale, w_out):
    b, t, _ = h.shape
    proj = h @ w_in
    glu = proj[..., :CONV_W] * jax.nn.sigmoid(proj[..., CONV_W:2 * CONV_W])
    u = proj[..., 2 * CONV_W:]
    c_in = jnp.concatenate([conv_ctx.astype(glu.dtype), glu], axis=1)
    c = lax.conv_general_dilated(c_in, w_dw[:, None, :].astype(c_in.dtype), window_strides=(1,),
                                 padding='VALID', dimension_numbers=('NWC', 'WIO', 'NWC'),
                                 feature_group_count=CONV_W) + b_dw
    c = jax.nn.silu(layer_norm(c, ln_g, ln_b))
    p_in = jnp.concatenate([pool_ctx.astype(u.dtype), u], axis=1)
    d = multiscale_pool(p_in, pos).reshape(b, t, POOL_GROUPS, POOL_GROUP_W)
    d = jnp.einsum('btgc,gcd->btgd', d, w_pool).reshape(b, t, POOL_W) * pool_scale
    out = jnp.concatenate([c, d], axis=-1) @ w_out
    return out, c_in[:, -(CONV_K - 1):], p_in[:, -POOL_CTX:]


def swiglu(x, wg, wu, wd):
    return (jax.nn.silu(x @ wg) * (x @ wu)) @ wd


def route(x2, w_router, bias):
    n = x2.shape[0]
    s = jax.nn.sigmoid((x2 @ w_router).astype(jnp.float32))
    s_sel = s + bias.astype(jnp.float32)
    grp = s_sel.reshape(n, N_EXPERT_GROUPS, N_EXPERTS // N_EXPERT_GROUPS)
    grp_score = lax.top_k(grp, 2)[0].sum(-1)
    _, top_g = lax.top_k(grp_score, TOPK_GROUPS)
    gmask = jax.nn.one_hot(top_g, N_EXPERT_GROUPS, dtype=jnp.float32).sum(1) > 0
    emask = jnp.repeat(gmask, N_EXPERTS // N_EXPERT_GROUPS, axis=1)
    _, idx = lax.top_k(jnp.where(emask, s_sel, -jnp.inf), TOP_K)
    gate = jnp.take_along_axis(s, idx, axis=1)
    gate = gate / gate.sum(-1, keepdims=True) * ROUTED_SCALE
    return idx, gate


def routed_experts(x2, idx, gate, w_g, w_u, w_d):
    n, d = x2.shape
    a = n * TOP_K
    e_flat = idx.reshape(a)
    tok_flat = jnp.repeat(jnp.arange(n, dtype=jnp.int32), TOP_K)
    g_flat = gate.reshape(a)
    order = jnp.argsort(e_flat)
    e_s, tok_s, g_s = e_flat[order], tok_flat[order], g_flat[order]
    counts = jnp.zeros((N_EXPERTS,), jnp.int32).at[e_flat].add(1)
    padded = (counts + MOE_BLOCK - 1) // MOE_BLOCK * MOE_BLOCK
    starts = jnp.cumsum(counts) - counts
    pends = jnp.cumsum(padded)
    pstarts = pends - padded
    dest = pstarts[e_s] + jnp.arange(a, dtype=jnp.int32) - starts[e_s]
    n_blocks = -(-a // MOE_BLOCK) + N_EXPERTS
    rows = n_blocks * MOE_BLOCK
    tok_buf = jnp.full((rows,), n, jnp.int32).at[dest].set(tok_s)
    g_buf = jnp.zeros((rows,), jnp.float32).at[dest].set(g_s)
    blk_e = jnp.minimum(jnp.searchsorted(pends, jnp.arange(n_blocks, dtype=jnp.int32) * MOE_BLOCK,
                                         side='right'), N_EXPERTS - 1)
    x_pad = jnp.concatenate([x2, jnp.zeros((1, d), x2.dtype)], axis=0)

    def block(args):
        tok, g, e = args
        return swiglu(x_pad[tok], w_g[e], w_u[e], w_d[e]) * g[:, None].astype(x2.dtype)

    y = lax.map(block, (tok_buf.reshape(n_blocks, MOE_BLOCK), g_buf.reshape(n_blocks, MOE_BLOCK), blk_e))
    out = jnp.zeros((n + 1, d), y.dtype).at[tok_buf].add(y.reshape(rows, d))
    return out[:n]


def moe_ffn(h, w_router, router_bias, w_eg, w_eu, w_ed, w_sg, w_su, w_sd):
    b, t, d = h.shape
    x2 = h.reshape(b * t, d)
    idx, gate = route(x2, w_router, router_bias)
    y = swiglu(x2, w_sg, w_su, w_sd) + routed_experts(x2, idx, gate, w_eg, w_eu, w_ed)
    return y.reshape(b, t, d)


def per_layer_embedding(h, p, w_proj, w_gate):
    return jax.nn.sigmoid(h @ w_gate) * (p.astype(h.dtype) @ w_proj)


def setup_inputs(seed: int = 0) -> dict:
    key = jax.random.key(seed)
    ks = jax.random.split(key, 40)

    def nrm(i, shape, scale):
        return jax.random.normal(ks[i], shape, jnp.float32) * scale

    n_pages = PAST_LEN // PAGE_SIZE
    n_used = DEC_BATCH * n_pages
    n_pool = n_used + n_used // 4
    page_table = jax.random.permutation(ks[0], n_pool)[:n_used].reshape(DEC_BATCH, n_pages).astype(jnp.int32)
    return {
        'x_prompt': nrm(1, (BATCH, SEQ, D_MODEL), 1.0),
        'x_sample': nrm(2, (DEC_BATCH, DEC_SEQ, D_MODEL), 1.0),
        'cache_k': nrm(3, (N_ATT_LAYERS, n_pool, PAGE_SIZE, H_ATT, HEAD_DIM), 1.0),
        'cache_v': nrm(4, (N_ATT_LAYERS, n_pool, PAGE_SIZE, H_ATT, HEAD_DIM), 1.0),
        'cache_logf': jax.nn.log_sigmoid(2.0 + nrm(5, (N_ATT_LAYERS, n_pool, PAGE_SIZE, H_FOX), 1.0)),
        'state_conv': nrm(6, (N_CONV_LAYERS, DEC_BATCH, CONV_K - 1, CONV_W), 0.5),
        'state_pool': nrm(7, (N_CONV_LAYERS, DEC_BATCH, POOL_CTX, POOL_W), 1.0),
        'page_table': page_table,
        'p_prompt': nrm(8, (DEPTH, BATCH, SEQ, PLE_DIM), 1.0),
        'p_sample': nrm(9, (DEPTH, DEC_BATCH, DEC_SEQ, PLE_DIM), 1.0),
        'w_att_in': nrm(10, (N_ATT_LAYERS, D_MODEL, 3 * ATT_W + H_FOX), D_MODEL ** -0.5),
        'b_forget': 2.0 + nrm(11, (N_ATT_LAYERS, H_FOX), 0.5),
        'w_att_out': nrm(12, (N_ATT_LAYERS, ATT_W, D_MODEL), ATT_W ** -0.5 * BETA),
        'w_cp_in': nrm(13, (N_CONV_LAYERS, D_MODEL, 2 * CONV_W + POOL_W), D_MODEL ** -0.5),
        'w_dw': nrm(14, (N_CONV_LAYERS, CONV_K, CONV_W), CONV_K ** -0.5),
        'b_dw': nrm(15, (N_CONV_LAYERS, CONV_W), 0.02),
        'ln_conv_g': 1.0 + nrm(16, (N_CONV_LAYERS, CONV_W), 0.05),
        'ln_conv_b': nrm(17, (N_CONV_LAYERS, CONV_W), 0.02),
        'w_pool': nrm(18, (N_CONV_LAYERS, POOL_GROUPS, POOL_GROUP_W, POOL_GROUP_W), POOL_GROUP_W ** -0.5),
        'pool_scale': 1.0 + nrm(19, (N_CONV_LAYERS, POOL_W), 0.1),
        'w_cp_out': nrm(20, (N_CONV_LAYERS, CONV_W + POOL_W, D_MODEL), (CONV_W + POOL_W) ** -0.5 * BETA),
        'ln_g': 1.0 + nrm(21, (DEPTH, 2, D_MODEL), 0.05),
        'ln_b': nrm(22, (DEPTH, 2, D_MODEL), 0.02),
        'w_router': nrm(23, (DEPTH, D_MODEL, N_EXPERTS), D_MODEL ** -0.5),
        'router_bias': nrm(24, (DEPTH, N_EXPERTS), 0.01),
        'w_exp_gate': nrm(25, (DEPTH, N_EXPERTS, D_MODEL, EXPERT_FF), D_MODEL ** -0.5),
        'w_exp_up': nrm(26, (DEPTH, N_EXPERTS, D_MODEL, EXPERT_FF), D_MODEL ** -0.5),
        'w_exp_down': nrm(27, (DEPTH, N_EXPERTS, EXPERT_FF, D_MODEL), EXPERT_FF ** -0.5 * BETA),
        'w_sh_gate': nrm(28, (DEPTH, D_MODEL, SHARED_FF), D_MODEL ** -0.5),
        'w_sh_up': nrm(29, (DEPTH, D_MODEL, SHARED_FF), D_MODEL ** -0.5),
        'w_sh_down': nrm(30, (DEPTH, SHARED_FF, D_MODEL), SHARED_FF ** -0.5 * BETA),
        'w_ple_proj': nrm(31, (DEPTH, PLE_DIM, D_MODEL), PLE_DIM ** -0.5),
        'w_ple_gate': nrm(32, (DEPTH, D_MODEL, D_MODEL), D_MODEL ** -0.5),
    }


def reference(x_prompt, x_sample, cache_k, cache_v, cache_logf, state_conv, state_pool, page_table,
              p_prompt, p_sample, w_att_in, b_forget, w_att_out, w_cp_in, w_dw, b_dw, ln_conv_g, ln_conv_b,
              w_pool, pool_scale, w_cp_out, ln_g, ln_b, w_router, router_bias, w_exp_gate, w_exp_up,
              w_exp_down, w_sh_gate, w_sh_up, w_sh_down, w_ple_proj, w_ple_gate):
    xp, xs = x_prompt, x_sample
    bp, tp, _ = xp.shape
    ts = xs.shape[1]
    past = page_table.shape[1] * cache_k.shape[2]
    pos_p = jnp.arange(tp)
    pos_s = past + jnp.arange(ts)
    k_p, v_p, lf_p, k_s, v_s, lf_s = [], [], [], [], [], []
    cv_p, cv_s, pl_p, pl_s = [], [], [], []
    for i in range(DEPTH):
        j = i // 2
        if i % 2 == 0:
            mp, kk, vv, lf = attention_prompt(xp, w_att_in[j], b_forget[j], w_att_out[j])
            k_p.append(kk); v_p.append(vv); lf_p.append(lf)
            ms, kk, vv, lf = attention_sample(xs, cache_k, cache_v, cache_logf, j, page_table,
                                              w_att_in[j], b_forget[j], w_att_out[j])
            k_s.append(kk); v_s.append(vv); lf_s.append(lf)
        else:
            cp = (w_cp_in[j], w_dw[j], b_dw[j], ln_conv_g[j], ln_conv_b[j], w_pool[j], pool_scale[j], w_cp_out[j])
            mp, cst, pst = conv_pool_mixer(xp, jnp.zeros((bp, CONV_K - 1, CONV_W), xp.dtype),
                                           jnp.zeros((bp, POOL_CTX, POOL_W), xp.dtype), pos_p, *cp)
            cv_p.append(cst); pl_p.append(pst)
            ms, cst, pst = conv_pool_mixer(xs, state_conv[j], state_pool[j], pos_s, *cp)
            cv_s.append(cst); pl_s.append(pst)
        xp = post_norm(xp, mp, ln_g[i, 0], ln_b[i, 0])
        xs = post_norm(xs, ms, ln_g[i, 0], ln_b[i, 0])
        moe = (w_router[i], router_bias[i], w_exp_gate[i], w_exp_up[i], w_exp_down[i],
               w_sh_gate[i], w_sh_up[i], w_sh_down[i])
        xp = post_norm(xp, moe_ffn(xp, *moe), ln_g[i, 1], ln_b[i, 1])
        xs = post_norm(xs, moe_ffn(xs, *moe), ln_g[i, 1], ln_b[i, 1])
        xp = xp + per_layer_embedding(xp, p_prompt[i], w_ple_proj[i], w_ple_gate[i])
        xs = xs + per_layer_embedding(xs, p_sample[i], w_ple_proj[i], w_ple_gate[i])
    return (xp, xs, jnp.stack(k_p), jnp.stack(v_p), jnp.stack(lf_p), jnp.stack(k_s), jnp.stack(v_s),
            jnp.stack(lf_s), jnp.stack(cv_p), jnp.stack(cv_s), jnp.stack(pl_p), jnp.stack(pl_s))
```

```python
import functools

import jax
import jax.numpy as jnp
from jax import lax
from jax.experimental import pallas as pl
from jax.experimental.pallas import tpu as pltpu

F32 = jnp.float32
BF16 = jnp.bfloat16

D_MODEL = 1024
HEAD_DIM = 64
H_SB = 8
H_FOX = 8
ATT_W = 1024
CONV_W = 512
CONV_K = 31
POOL_W = 512
POOL_WINDOWS = (2, 4, 8, 16)
POOL_GROUP_W = 128
POOL_CTX = 15
N_EXPERTS = 64
TOP_K = 8
N_EXPERT_GROUPS = 8
TOPK_GROUPS = 4
EXPERT_FF = 256
ROUTED_SCALE = 2.5
PLE_DIM = 256
PAGE_SIZE = 128
DEPTH = 2
ALPHA = (2 * DEPTH) ** 0.25
LN_EPS = 1e-5

LANES = 128
HEADS_PER_BLOCK = LANES // HEAD_DIM
VMEM_LIMIT = 48 * 1024 * 1024

ATT_TQ = 256
ATT_TK = 256
MOE_BM = 256
PAGES_PER_STEP = 8
NEG_BIG = -1e30


def _params(n_axes):
    return pltpu.CompilerParams(dimension_semantics=("arbitrary",) * n_axes,
                                vmem_limit_bytes=VMEM_LIMIT)


def _log_sigmoid(x):
    return jnp.minimum(x, 0.0) - jnp.log(1.0 + jnp.exp(-jnp.abs(x)))


def _split2(x):
    hi = x.astype(BF16)
    lo = (x - hi.astype(F32)).astype(BF16)
    return hi, lo


def _split3(x):
    hi = x.astype(BF16)
    r = x - hi.astype(F32)
    mid = r.astype(BF16)
    lo = (r - mid.astype(F32)).astype(BF16)
    return hi, mid, lo


def _dot(a, b):
    return jnp.dot(a, b, preferred_element_type=F32)


def _dot_nt(a, b):
    return lax.dot_general(a, b, (((1,), (1,)), ((), ())), preferred_element_type=F32)


def _iota(shape, dim):
    return lax.broadcasted_iota(jnp.int32, shape, dim)


def _layer_norm(y, g, b):
    mu = jnp.mean(y, axis=-1, keepdims=True)
    yc = y - mu
    var = jnp.mean(yc * yc, axis=-1, keepdims=True)
    return yc * lax.rsqrt(var + LN_EPS) * g + b


def _full(shape):
    n = len(shape)
    return pl.BlockSpec(shape, lambda *_: (0,) * n)


def _att_proj_kernel(x_ref, wqkv_ref, wf_ref, wft_ref, bf_ref, bft_ref,
                     q_ref, k_ref, v_ref, kb_ref, vb_ref, lf_ref, lft_ref):
    xb = x_ref[...].astype(BF16)
    qkv = _dot(xb, wqkv_ref[...])
    q_ref[...] = (qkv[:, :ATT_W] * (HEAD_DIM ** -0.5)).astype(BF16)
    k = qkv[:, ATT_W:2 * ATT_W]
    v = qkv[:, 2 * ATT_W:]
    k_ref[...] = k
    v_ref[...] = v
    kb_ref[...] = k.astype(BF16)
    vb_ref[...] = v.astype(BF16)
    lf_ref[...] = _log_sigmoid(_dot(xb, wf_ref[...]) + bf_ref[...])
    lft_ref[...] = _log_sigmoid(_dot_nt(wft_ref[...], xb) + bft_ref[...])


def _att_proj(x, wqkv, wf, wft, bf, bft, tm):
    n = x.shape[0]
    row = lambda w: pl.BlockSpec((tm, w), lambda i: (i, 0))
    return pl.pallas_call(
        _att_proj_kernel,
        grid=(n // tm,),
        in_specs=[row(D_MODEL), _full(wqkv.shape), _full(wf.shape), _full(wft.shape),
                  _full(bf.shape), _full(bft.shape)],
        out_specs=[row(ATT_W), row(ATT_W), row(ATT_W), row(ATT_W), row(ATT_W), row(H_FOX),
                   pl.BlockSpec((H_FOX, tm), lambda i: (0, i))],
        out_shape=[jax.ShapeDtypeStruct((n, ATT_W), BF16),
                   jax.ShapeDtypeStruct((n, ATT_W), F32),
                   jax.ShapeDtypeStruct((n, ATT_W), F32),
                   jax.ShapeDtypeStruct((n, ATT_W), BF16),
                   jax.ShapeDtypeStruct((n, ATT_W), BF16),
                   jax.ShapeDtypeStruct((n, H_FOX), F32),
                   jax.ShapeDtypeStruct((H_FOX, n), F32)],
        compiler_params=_params(1),
        name="att_proj",
    )(x, wqkv, wf, wft, bf, bft)


def _fcum_kernel(lf_ref, lft_ref, fq_ref, fk_ref):
    t = lf_ref.shape[1]
    ch = ATT_TK
    lower = (_iota((ch, ch), 1) <= _iota((ch, ch), 0)).astype(BF16)
    upper = (_iota((ch, ch), 0) <= _iota((ch, ch), 1)).astype(BF16)
    carry_c = jnp.zeros((1, H_FOX), F32)
    carry_r = jnp.zeros((H_FOX, 1), F32)
    for c in range(t // ch):
        xc = lf_ref[0, c * ch:(c + 1) * ch, :]
        cs = carry_c
        for part in _split3(xc):
            cs = cs + _dot(lower, part)
        fq_ref[0, c * ch:(c + 1) * ch, :] = cs
        carry_c = cs[ch - 1:ch, :]
        xr = lft_ref[:, c * ch:(c + 1) * ch]
        rs = carry_r
        for part in _split3(xr):
            rs = rs + _dot(part, upper)
        fk_ref[0, c] = rs
        carry_r = rs[:, ch - 1:ch]


def _fcum(lf, lft, batch, t):
    nk = t // ATT_TK
    return pl.pallas_call(
        _fcum_kernel,
        grid=(batch,),
        in_specs=[pl.BlockSpec((1, t, H_FOX), lambda b: (b, 0, 0)),
                  pl.BlockSpec((H_FOX, t), lambda b: (0, b))],
        out_specs=[pl.BlockSpec((1, t, H_FOX), lambda b: (b, 0, 0)),
                   pl.BlockSpec((1, nk, H_FOX, ATT_TK), lambda b: (b, 0, 0, 0))],
        out_shape=[jax.ShapeDtypeStruct((batch, t, H_FOX), F32),
                   jax.ShapeDtypeStruct((batch, nk, H_FOX, ATT_TK), F32)],
        compiler_params=_params(1),
        name="fcum",
    )(lf, lft)


def _head_masked(q):
    lane = _iota(q.shape, 1)
    zero = jnp.zeros_like(q)
    return [jnp.where((lane >= h * HEAD_DIM) & (lane < (h + 1) * HEAD_DIM), q, zero)
            for h in range(HEADS_PER_BLOCK)]


def _merge_heads(parts):
    lane = _iota(parts[0].shape, 1)
    out = parts[0]
    for h in range(1, HEADS_PER_BLOCK):
        out = jnp.where(lane >= h * HEAD_DIM, parts[h], out)
    return out


def _sb_attn_kernel(q_ref, k_ref, v_ref, o_ref, acc_ref):
    i = pl.program_id(2)
    tq, tk = ATT_TQ, ATT_TK
    q_heads = _head_masked(q_ref[0])
    after = (_iota((tk, tk), 0) > _iota((tk, tk), 1)).astype(BF16)
    row = _iota((tq, tk), 0)
    col = _iota((tq, tk), 1)
    acc_ref[...] = jnp.zeros_like(acc_ref)

    def body(jj, carry):
        j = i - jj
        start = pl.multiple_of(j * tk, tk)
        kb = k_ref[0, pl.ds(start, tk), :]
        vb = v_ref[0, pl.ds(start, tk), :]
        strict = (col + j * tk) < (row + i * tq)
        new_carry = []
        for h in range(HEADS_PER_BLOCK):
            z = _dot_nt(q_heads[h], kb)
            ls = _log_sigmoid(z)
            lk = jnp.where(strict, ls - z, 0.0)
            hi, lo = _split2(lk)
            between = _dot(hi, after) + _dot(lo, after) + carry[h]
            w = jnp.where(strict, jnp.exp(ls + between), 0.0)
            acc_ref[h] += _dot(w.astype(BF16), vb)
            new_carry.append(carry[h] + jnp.sum(lk, axis=1, keepdims=True))
        return tuple(new_carry)

    zero = jnp.zeros((tq, 1), F32)
    lax.fori_loop(0, i + 1, body, (zero,) * HEADS_PER_BLOCK)
    o_ref[0] = _merge_heads([acc_ref[h] for h in range(HEADS_PER_BLOCK)]).astype(BF16)


def _fox_attn_kernel(q_ref, k_ref, v_ref, fq_ref, fk_ref, o_ref, acc_ref):
    p = pl.program_id(1)
    i = pl.program_id(2)
    tq, tk = ATT_TQ, ATT_TK
    q_heads = _head_masked(q_ref[0])
    row = _iota((tq, tk), 0)
    col = _iota((tq, tk), 1)
    fq_all = fq_ref[0]
    lane8 = _iota(fq_all.shape, 1)
    sub8 = _iota((H_FOX, tk), 0)
    fq_heads = [jnp.sum(jnp.where(lane8 == p * HEADS_PER_BLOCK + h, fq_all, 0.0), axis=1,
                        keepdims=True) for h in range(HEADS_PER_BLOCK)]
    acc_ref[...] = jnp.zeros_like(acc_ref)

    def body(jj, carry):
        j = i - jj
        start = pl.multiple_of(j * tk, tk)
        kb = k_ref[0, pl.ds(start, tk), :]
        vb = v_ref[0, pl.ds(start, tk), :]
        fk_all = fk_ref[0, j]
        incl = (col + j * tk) <= (row + i * tq)
        new_carry = []
        for h in range(HEADS_PER_BLOCK):
            m_old, l_old = carry[2 * h], carry[2 * h + 1]
            f = p * HEADS_PER_BLOCK + h
            fk = jnp.sum(jnp.where(sub8 == f, fk_all, 0.0), axis=0, keepdims=True)
            s = jnp.where(incl, _dot_nt(q_heads[h], kb) + (fq_heads[h] - fk), NEG_BIG)
            m_new = jnp.maximum(m_old, jnp.max(s, axis=1, keepdims=True))
            alpha = jnp.exp(m_old - m_new)
            pr = jnp.exp(s - m_new)
            acc_ref[h] = alpha * acc_ref[h] + _dot(pr.astype(BF16), vb)
            new_carry += [m_new, alpha * l_old + jnp.sum(pr, axis=1, keepdims=True)]
        return tuple(new_carry)

    init = (jnp.full((tq, 1), NEG_BIG, F32), jnp.zeros((tq, 1), F32)) * HEADS_PER_BLOCK
    fin = lax.fori_loop(0, i + 1, body, init)
    outs = [acc_ref[h] / fin[2 * h + 1] for h in range(HEADS_PER_BLOCK)]
    o_ref[0] = _merge_heads(outs).astype(BF16)


def _prompt_attention(qb, kb, vb, fq, fk, batch, t):
    n_pairs = H_SB // HEADS_PER_BLOCK
    nq = t // ATT_TQ
    q3, k3, v3 = (a.reshape(batch, t, ATT_W) for a in (qb, kb, vb))
    scratch = [pltpu.VMEM((HEADS_PER_BLOCK, ATT_TQ, LANES), F32)]
    out_shape = jax.ShapeDtypeStruct((batch, t, n_pairs * LANES), BF16)

    def specs(off):
        return [pl.BlockSpec((1, ATT_TQ, LANES), lambda b, p, i: (b, i, p + off)),
                pl.BlockSpec((1, t, LANES), lambda b, p, i: (b, 0, p + off)),
                pl.BlockSpec((1, t, LANES), lambda b, p, i: (b, 0, p + off))]

    out_spec = pl.BlockSpec((1, ATT_TQ, LANES), lambda b, p, i: (b, i, p))
    o_sb = pl.pallas_call(
        _sb_attn_kernel, grid=(batch, n_pairs, nq), in_specs=specs(0), out_specs=out_spec,
        out_shape=out_shape, scratch_shapes=scratch, compiler_params=_params(3), name="sb_attn",
    )(q3, k3, v3)
    o_fx = pl.pallas_call(
        _fox_attn_kernel, grid=(batch, n_pairs, nq),
        in_specs=specs(n_pairs) + [
            pl.BlockSpec((1, ATT_TQ, H_FOX), lambda b, p, i: (b, i, 0)),
            pl.BlockSpec((1, t // ATT_TK, H_FOX, ATT_TK), lambda b, p, i: (b, 0, 0, 0))],
        out_specs=out_spec, out_shape=out_shape, scratch_shapes=scratch,
        compiler_params=_params(3), name="fox_attn",
    )(q3, k3, v3, fq, fk)
    return o_sb.reshape(batch * t, -1), o_fx.reshape(batch * t, -1)


def _decode_attn_kernel(pt_ref, q_ref, kn_ref, vn_ref, lfn_ref, *refs):
    pps = PAGES_PER_STEP
    k_refs, v_refs, lf_refs = refs[:pps], refs[pps:2 * pps], refs[2 * pps:3 * pps]
    o_ref, acc_ref, m_ref, l_ref, c_ref = refs[3 * pps:]
    c = pl.program_id(1)
    nh = H_SB + H_FOX
    head_row = _iota((nh, ATT_W), 0)
    head_lane = _iota((nh, ATT_W), 1) // HEAD_DIM
    own = head_row == head_lane
    qblk = jnp.where(own, jnp.broadcast_to(q_ref[0].astype(F32), (nh, ATT_W)), 0.0).astype(BF16)
    is_fox = _iota((nh, 1), 0) >= H_SB
    after = (_iota((PAGE_SIZE, PAGE_SIZE), 0) > _iota((PAGE_SIZE, PAGE_SIZE), 1)).astype(BF16)

    @pl.when(c == 0)
    def _():
        kn = jnp.broadcast_to(kn_ref[0], (8, ATT_W)).astype(BF16)
        z_new = _dot_nt(qblk, kn)[:, 0:1]
        m_ref[...] = jnp.where(is_fox, z_new, 0.0)
        l_ref[...] = jnp.where(is_fox, 1.0, 0.0)
        acc_ref[...] = jnp.where(is_fox, jnp.broadcast_to(vn_ref[0], (nh, ATT_W)), 0.0)
        c_ref[...] = jnp.concatenate([jnp.zeros((H_SB, 1), F32), lfn_ref[0]], axis=0)

    for r in reversed(range(pps)):
        kp = k_refs[r][0].astype(BF16)
        vp = v_refs[r][0].astype(BF16)
        z = _dot_nt(qblk, kp)
        carry = c_ref[...]
        z_sb = z[:H_SB]
        ls = _log_sigmoid(z_sb)
        lk = ls - z_sb
        hi, lo = _split2(lk)
        between = _dot(hi, after) + _dot(lo, after) + carry[:H_SB]
        w_sb = jnp.exp(ls + between)
        lft = lf_refs[r][0]
        g = carry[H_SB:]
        for part in _split3(lft):
            g = g + _dot(part, after)
        s = z[H_SB:] + g
        m_old = m_ref[...][H_SB:]
        m_new = jnp.maximum(m_old, jnp.max(s, axis=1, keepdims=True))
        alpha = jnp.exp(m_old - m_new)
        pr = jnp.exp(s - m_new)
        w = jnp.concatenate([w_sb, pr], axis=0).astype(BF16)
        scale = jnp.concatenate([jnp.ones((H_SB, 1), F32), alpha], axis=0)
        acc_ref[...] = scale * acc_ref[...] + _dot(w, vp)
        l_ref[...] = scale * l_ref[...] + jnp.concatenate(
            [jnp.zeros((H_SB, 1), F32), jnp.sum(pr, axis=1, keepdims=True)], axis=0)
        m_ref[...] = jnp.concatenate([jnp.zeros((H_SB, 1), F32), m_new], axis=0)
        c_ref[...] = carry + jnp.concatenate(
            [jnp.sum(lk, axis=1, keepdims=True), jnp.sum(lft, axis=1, keepdims=True)], axis=0)

    @pl.when(c == pl.num_programs(1) - 1)
    def _():
        denom = jnp.where(is_fox, l_ref[...], 1.0)
        o = jnp.where(own, acc_ref[...] / denom, 0.0)
        o_ref[0] = jnp.sum(o, axis=0, keepdims=True).astype(BF16)


def _decode_attention(qs, ks, vs, lfs, ck, cv, clft, page_table):
    nseq, n_pages = page_table.shape
    pps = PAGES_PER_STEP
    n_steps = n_pages // pps
    nh = H_SB + H_FOX

    def page_spec(shape, r):
        return pl.BlockSpec(
            shape, lambda b, c, pt: (pt[b, (n_steps - 1 - c) * pps + r], 0, 0))

    per_seq = lambda shape: pl.BlockSpec(shape, lambda b, c, pt: (b, 0, 0))
    in_specs = [per_seq((1, 1, ATT_W)), per_seq((1, 1, ATT_W)), per_seq((1, 1, ATT_W)),
                per_seq((1, H_FOX, 1))]
    in_specs += [page_spec((1, PAGE_SIZE, ATT_W), r) for r in range(pps)]
    in_specs += [page_spec((1, PAGE_SIZE, ATT_W), r) for r in range(pps)]
    in_specs += [page_spec((1, H_FOX, PAGE_SIZE), r) for r in range(pps)]
    grid_spec = pltpu.PrefetchScalarGridSpec(
        num_scalar_prefetch=1, grid=(nseq, n_steps), in_specs=in_specs,
        out_specs=per_seq((1, 1, ATT_W)),
        scratch_shapes=[pltpu.VMEM((nh, ATT_W), F32), pltpu.VMEM((nh, 1), F32),
                        pltpu.VMEM((nh, 1), F32), pltpu.VMEM((nh, 1), F32)])
    out = pl.pallas_call(
        _decode_attn_kernel, grid_spec=grid_spec,
        out_shape=jax.ShapeDtypeStruct((nseq, 1, ATT_W), BF16),
        compiler_params=_params(2), name="decode_attn",
    )(page_table, qs.reshape(nseq, 1, ATT_W), ks.reshape(nseq, 1, ATT_W),
      vs.reshape(nseq, 1, ATT_W), lfs.reshape(nseq, H_FOX, 1),
      *([ck] * pps), *([cv] * pps), *([clft] * pps))
    return out.reshape(nseq, ATT_W)


def _proj_ln_kernel(x_ref, a_ref, b_ref, wa_ref, wb_ref, g_ref, beta_ref, o_ref):
    y = ALPHA * x_ref[...] + _dot(a_ref[...], wa_ref[...]) + _dot(b_ref[...], wb_ref[...])
    o_ref[...] = _layer_norm(y, g_ref[...], beta_ref[...])


def _proj_ln(x, a, b, wa, wb, g, beta, tm):
    n = x.shape[0]
    row = lambda w: pl.BlockSpec((tm, w), lambda i: (i, 0))
    return pl.pallas_call(
        _proj_ln_kernel, grid=(n // tm,),
        in_specs=[row(D_MODEL), row(a.shape[1]), row(b.shape[1]), _full(wa.shape), _full(wb.shape),
                  _full(g.shape), _full(beta.shape)],
        out_specs=row(D_MODEL), out_shape=jax.ShapeDtypeStruct((n, D_MODEL), F32),
        compiler_params=_params(1), name="proj_ln",
    )(x, a, b, wa, wb, g, beta)


def _first_argmax(v, idx_rows):
    m = jnp.max(v, axis=0, keepdims=True)
    first = jnp.min(jnp.where(v == m, idx_rows, float(v.shape[0])), axis=0, keepdims=True)
    return m, first


def _set_row(acc, rows, k, value):
    return jnp.where(rows == k, jnp.broadcast_to(value, acc.shape), acc)


def _router_kernel(x_ref, wh_ref, wm_ref, bias_ref, cin_ref, idx_ref, gate_ref, pos_ref, cout_ref,
                   cnt_ref):
    step = pl.program_id(0)
    tm = x_ref.shape[0]
    ne, ng = N_EXPERTS, N_EXPERT_GROUPS
    per_g = ne // ng

    @pl.when(step == 0)
    def _():
        cnt_ref[...] = cin_ref[...]

    x = x_ref[...]
    xh, xm = _split2(x)
    lhs = jnp.concatenate([wh_ref[...], wm_ref[...], wh_ref[...]], axis=1)
    rhs = jnp.concatenate([xh, xh, xm], axis=1)
    s = jax.nn.sigmoid(_dot_nt(lhs, rhs))
    s_sel = s + bias_ref[...]
    e_rows = _iota((ne, tm), 0).astype(F32)
    k_rows = _iota((TOP_K, tm), 0)
    g_rows = _iota((ng, tm), 0)
    g_rows_f = g_rows.astype(F32)
    sub = _iota((per_g, tm), 0).astype(F32)
    gs = jnp.zeros((ng, tm), F32)
    for g in range(ng):
        blk = s_sel[g * per_g:(g + 1) * per_g]
        m1, a1 = _first_argmax(blk, sub)
        m2 = jnp.max(jnp.where(sub == a1, -jnp.inf, blk), axis=0, keepdims=True)
        gs = _set_row(gs, g_rows, g, m1 + m2)
    chosen_g = jnp.zeros((ng, tm), F32)
    for _ in range(TOPK_GROUPS):
        _, a = _first_argmax(gs, g_rows_f)
        hit = g_rows_f == a
        chosen_g = jnp.where(hit, 1.0, chosen_g)
        gs = jnp.where(hit, -jnp.inf, gs)
    emask = jnp.concatenate(
        [jnp.broadcast_to(chosen_g[g:g + 1], (per_g, tm)) for g in range(ng)], axis=0)
    cand = jnp.where(emask > 0.5, s_sel, -jnp.inf)
    sel = jnp.zeros((ne, tm), F32)
    idx = jnp.zeros((TOP_K, tm), F32)
    gates = jnp.zeros((TOP_K, tm), F32)
    hits = []
    for k in range(TOP_K):
        _, a = _first_argmax(cand, e_rows)
        hit = e_rows == a
        hits.append(hit)
        sel = jnp.where(hit, 1.0, sel)
        cand = jnp.where(hit, -jnp.inf, cand)
        idx = _set_row(idx, k_rows, k, a)
        gates = _set_row(gates, k_rows, k, jnp.sum(jnp.where(hit, s, 0.0), axis=0, keepdims=True))
    gates = gates / jnp.sum(gates, axis=0, keepdims=True) * ROUTED_SCALE
    before = (_iota((tm, tm), 0) < _iota((tm, tm), 1)).astype(BF16)
    rank = _dot(sel.astype(BF16), before) + cnt_ref[...]
    pos = jnp.zeros((TOP_K, tm), F32)
    for k in range(TOP_K):
        pos = _set_row(pos, k_rows, k,
                       jnp.sum(jnp.where(hits[k], rank, 0.0), axis=0, keepdims=True))
    idx_ref[...] = idx.astype(jnp.int32)
    gate_ref[...] = gates
    pos_ref[...] = pos.astype(jnp.int32)
    cnt_ref[...] += jnp.sum(sel, axis=1, keepdims=True)
    cout_ref[...] = cnt_ref[...]


def _router(x, wh, wm, bias, counts_in, tm):
    n = x.shape[0]
    col = lambda: pl.BlockSpec((TOP_K, tm), lambda i: (0, i))
    return pl.pallas_call(
        _router_kernel, grid=(n // tm,),
        in_specs=[pl.BlockSpec((tm, D_MODEL), lambda i: (i, 0)), _full(wh.shape), _full(wm.shape),
                  _full(bias.shape), _full(counts_in.shape)],
        out_specs=[col(), col(), col(), _full((N_EXPERTS, 1))],
        out_shape=[jax.ShapeDtypeStruct((TOP_K, n), jnp.int32),
                   jax.ShapeDtypeStruct((TOP_K, n), F32),
                   jax.ShapeDtypeStruct((TOP_K, n), jnp.int32),
                   jax.ShapeDtypeStruct((N_EXPERTS, 1), F32)],
        scratch_shapes=[pltpu.VMEM((N_EXPERTS, 1), F32)],
        compiler_params=_params(1), name="router",
    )(x, wh, wm, bias, counts_in)


def _row_copy_wait(src_rows, dst_hbm, sem, n_rows):
    for _ in range(TOP_K):
        pltpu.make_async_copy(src_rows, dst_hbm.at[pl.ds(0, n_rows)], sem).wait()


def _dispatch_rows(dest_ref, x_ref, xs_ref, sem):
    tm = x_ref.shape[0]

    def issue(n, _):
        for k in range(TOP_K):
            pltpu.make_async_copy(x_ref.at[pl.ds(n, 1)],
                                  xs_ref.at[pl.ds(dest_ref[0, k, n], 1)], sem).start()
        return 0

    lax.fori_loop(0, tm, issue, 0)
    _row_copy_wait(x_ref, xs_ref, sem, tm)


def _dispatch_kernel(dest_ref, dest_small_ref, x_ref, x_small_ref, xs_ref, sem):
    _dispatch_rows(dest_ref, x_ref, xs_ref, sem)

    @pl.when(pl.program_id(0) == 0)
    def _():
        _dispatch_rows(dest_small_ref, x_small_ref, xs_ref, sem)


def _dispatch(x, dest3, x_small, dest_small3, tm):
    n, n_small = x.shape[0], x_small.shape[0]
    return pl.pallas_call(
        _dispatch_kernel, grid=(n // tm,),
        in_specs=[pl.BlockSpec((1, TOP_K, tm), lambda i: (i, 0, 0), memory_space=pltpu.SMEM),
                  pl.BlockSpec((1, TOP_K, n_small), lambda i: (0, 0, 0), memory_space=pltpu.SMEM),
                  pl.BlockSpec((tm, D_MODEL), lambda i: (i, 0)),
                  pl.BlockSpec((n_small, D_MODEL), lambda i: (0, 0))],
        out_specs=pl.BlockSpec(memory_space=pl.ANY),
        out_shape=jax.ShapeDtypeStruct(((n + n_small) * TOP_K, D_MODEL), F32),
        scratch_shapes=[pltpu.SemaphoreType.DMA(())],
        compiler_params=_params(1), name="moe_dispatch",
    )(dest3, dest_small3, x, x_small)


def _experts_kernel(blk_ref, exp_ref, lo_ref, hi_ref, first_ref, xs_ref, wg_ref, wu_ref, wd_ref,
                    y_ref):
    s = pl.program_id(0)
    lo, hi = lo_ref[s], hi_ref[s]

    @pl.when(hi > lo)
    def _():
        xb = xs_ref[...].astype(BF16)
        gate = _dot(xb, wg_ref[0].astype(BF16))
        up = _dot(xb, wu_ref[0].astype(BF16))
        hidden = (gate * jax.nn.sigmoid(gate) * up).astype(BF16)
        y = _dot(hidden, wd_ref[0].astype(BF16))
        rows = _iota(y.shape, 0)
        y = jnp.where((rows >= lo) & (rows < hi), y, 0.0)

        @pl.when(first_ref[s] == 1)
        def _():
            y_ref[...] = y

        @pl.when(first_ref[s] == 0)
        def _():
            y_ref[...] += y


def _experts(xs, plan, wg, wu, wd):
    blk, exp, lo, hi, first = plan
    n_steps = blk.shape[0]
    grid_spec = pltpu.PrefetchScalarGridSpec(
        num_scalar_prefetch=5, grid=(n_steps,),
        in_specs=[pl.BlockSpec((MOE_BM, D_MODEL), lambda s, b, e, *_: (b[s], 0)),
                  pl.BlockSpec((1, D_MODEL, EXPERT_FF), lambda s, b, e, *_: (e[s], 0, 0)),
                  pl.BlockSpec((1, D_MODEL, EXPERT_FF), lambda s, b, e, *_: (e[s], 0, 0)),
                  pl.BlockSpec((1, EXPERT_FF, D_MODEL), lambda s, b, e, *_: (e[s], 0, 0))],
        out_specs=pl.BlockSpec((MOE_BM, D_MODEL), lambda s, b, e, *_: (b[s], 0)))
    return pl.pallas_call(
        _experts_kernel, grid_spec=grid_spec,
        out_shape=jax.ShapeDtypeStruct(xs.shape, F32),
        compiler_params=_params(1), name="moe_experts",
    )(blk, exp, lo, hi, first, xs, wg, wu, wd)


def _combine_kernel(dest_ref, x_ref, gate_ref, p_ref, ys_ref, wsg_ref, wsu_ref, wsd_ref, g_ref,
                    beta_ref, wpg_ref, wpp_ref, o_ref, ybuf, sem):
    tm = x_ref.shape[0]

    def issue(n, _):
        for k in range(TOP_K):
            pltpu.make_async_copy(ys_ref.at[pl.ds(dest_ref[0, k, n], 1)],
                                  ybuf.at[k, pl.ds(n, 1)], sem).start()
        return 0

    lax.fori_loop(0, tm, issue, 0)
    x = x_ref[...]
    xb = x.astype(BF16)
    gate = _dot(xb, wsg_ref[...])
    up = _dot(xb, wsu_ref[...])
    y = _dot((gate * jax.nn.sigmoid(gate) * up).astype(BF16), wsd_ref[...])
    for k in range(TOP_K):
        pltpu.make_async_copy(ys_ref.at[pl.ds(0, tm)], ybuf.at[k], sem).wait()
    gates = gate_ref[...]
    for k in range(TOP_K):
        y = y + ybuf[k] * gates[:, k:k + 1]
    x2 = _layer_norm(ALPHA * x + y, g_ref[...], beta_ref[...])
    emb = _dot(p_ref[...].astype(BF16), wpp_ref[...])
    o_ref[...] = x2 + jax.nn.sigmoid(_dot(x2.astype(BF16), wpg_ref[...])) * emb


def _combine(x, gates, p, dest3, ys, wsg, wsu, wsd, g, beta, wpg, wpp, tm):
    n = x.shape[0]
    row = lambda w: pl.BlockSpec((tm, w), lambda i: (i, 0))
    return pl.pallas_call(
        _combine_kernel, grid=(n // tm,),
        in_specs=[pl.BlockSpec((1, TOP_K, tm), lambda i: (i, 0, 0), memory_space=pltpu.SMEM),
                  row(D_MODEL), row(TOP_K), row(PLE_DIM), pl.BlockSpec(memory_space=pl.ANY),
                  _full(wsg.shape), _full(wsu.shape), _full(wsd.shape), _full(g.shape),
                  _full(beta.shape), _full(wpg.shape), _full(wpp.shape)],
        out_specs=row(D_MODEL), out_shape=jax.ShapeDtypeStruct((n, D_MODEL), F32),
        scratch_shapes=[pltpu.VMEM((TOP_K, tm, D_MODEL), F32), pltpu.SemaphoreType.DMA(())],
        compiler_params=_params(1), name="moe_combine",
    )(dest3, x, gates, p, ys, wsg, wsu, wsd, g, beta, wpg, wpp)


def _moe_plan(counts, n_rows):
    n_blocks = n_rows // MOE_BM
    n_steps = n_blocks + N_EXPERTS - 1
    starts = jnp.cumsum(counts) - counts
    ends = starts + counts
    first_blk = starts // MOE_BM
    last_blk = jnp.where(counts > 0, (ends - 1) // MOE_BM, first_blk - 1)
    n_pairs = last_blk - first_blk + 1
    pair_end = jnp.cumsum(n_pairs)
    pair_start = pair_end - n_pairs
    total = pair_end[-1]
    s = jnp.arange(n_steps, dtype=jnp.int32)
    valid = s < total
    s_eff = jnp.minimum(s, total - 1)
    e = jnp.minimum(jnp.searchsorted(pair_end, s_eff, side="right"), N_EXPERTS - 1).astype(jnp.int32)
    blk = (first_blk[e] + (s_eff - pair_start[e])).astype(jnp.int32)
    lo = jnp.clip(starts[e] - blk * MOE_BM, 0, MOE_BM)
    hi = jnp.clip(ends[e] - blk * MOE_BM, 0, MOE_BM)
    lo = jnp.where(valid, lo, 0).astype(jnp.int32)
    hi = jnp.where(valid, hi, 0).astype(jnp.int32)
    prev_blk = jnp.concatenate([jnp.full((1,), -1, jnp.int32), blk[:-1]])
    first = (valid & (blk != prev_blk)).astype(jnp.int32)
    return starts, (blk, e, lo, hi, first)


def _tile3(dest_t, tm):
    n = dest_t.shape[1]
    return dest_t.reshape(TOP_K, n // tm, tm).transpose(1, 0, 2)


def _moe_ple(groups, layer_w):
    (wrh, wrm, rbias, weg, weu, wed, wsg, wsu, wsd, g, beta, wpg, wpp) = layer_w
    counts = jnp.zeros((N_EXPERTS, 1), F32)
    routed = []
    for x1, _, tm in groups:
        idx_t, gate_t, pos_t, counts = _router(x1, wrh, wrm, rbias, counts, tm)
        routed.append((idx_t, gate_t, pos_t))
    n_rows = sum(x1.shape[0] for x1, _, _ in groups) * TOP_K
    starts, plan = _moe_plan(counts[:, 0].astype(jnp.int32), n_rows)
    dests = [_tile3(starts[idx_t] + pos_t, tm)
             for (_, _, tm), (idx_t, _, pos_t) in zip(groups, routed)]
    (x_big, _, tm_big), (x_small, _, _) = groups
    xs = _dispatch(x_big, dests[0], x_small, dests[1], tm_big)
    ys = _experts(xs, plan, weg, weu, wed)
    outs = []
    for (x1, p, tm), (_, gate_t, _), dest3 in zip(groups, routed, dests):
        tc = min(tm, 256)
        dest_c = dest3 if tc == tm else _tile3(dest3.transpose(1, 0, 2).reshape(TOP_K, -1), tc)
        outs.append(_combine(x1, gate_t.T, p, dest_c, ys, wsg, wsu, wsd, g, beta, wpg, wpp, tc))
    return outs


def _cp_proj_kernel(x_ref, w_ref, glu_ref, u_ref):
    proj = _dot(x_ref[...].astype(BF16), w_ref[...])
    glu_ref[...] = proj[:, :CONV_W] * jax.nn.sigmoid(proj[:, CONV_W:2 * CONV_W])
    u_ref[...] = proj[:, 2 * CONV_W:]


def _cp_proj(x, w, tm):
    n = x.shape[0]
    row = lambda wd: pl.BlockSpec((tm, wd), lambda i: (i, 0))
    return pl.pallas_call(
        _cp_proj_kernel, grid=(n // tm,), in_specs=[row(D_MODEL), _full(w.shape)],
        out_specs=[row(CONV_W), row(POOL_W)],
        out_shape=[jax.ShapeDtypeStruct((n, CONV_W), F32), jax.ShapeDtypeStruct((n, POOL_W), F32)],
        compiler_params=_params(1), name="cp_proj",
    )(x, w)


def _conv_tail(c, bdw, lng, lnb):
    c = _layer_norm(c + bdw, lng, lnb)
    return c * jax.nn.sigmoid(c)


def _pool_project(diffs, wpool_ref, scale):
    outs = [_dot(diffs[g].astype(BF16), wpool_ref[g]) for g in range(len(POOL_WINDOWS))]
    return jnp.concatenate(outs, axis=1) * scale


CONV_HALO = 32
POOL_HALO = 16


def _convpool_prompt_kernel(gc_ref, gp_ref, uc_ref, up_ref, wdw_ref, bdw_ref, lng_ref, lnb_ref,
                            wpool_ref, ps_ref, c_ref, d_ref):
    i = pl.program_id(1)
    tt = gc_ref.shape[1]
    keep = (i > 0).astype(F32)
    gwin = jnp.concatenate([gp_ref[0] * keep, gc_ref[0]], axis=0)
    wdw = wdw_ref[...]
    conv = jnp.zeros((tt, CONV_W), F32)
    for j in range(CONV_K):
        off = CONV_HALO - (CONV_K - 1) + j
        conv = conv + wdw[j:j + 1, :] * gwin[off:off + tt]
    c_ref[0] = _conv_tail(conv, bdw_ref[...], lng_ref[...], lnb_ref[...]).astype(BF16)
    uwin = jnp.concatenate([up_ref[0] * keep, uc_ref[0]], axis=0)
    pos = i * tt + _iota((tt, POOL_GROUP_W), 0)
    diffs = []
    for g, w in enumerate(POOL_WINDOWS):
        lanes = slice(g * POOL_GROUP_W, (g + 1) * POOL_GROUP_W)
        xg = uwin[:, lanes]
        tot = jnp.zeros((tt, POOL_GROUP_W), F32)
        for r in range(w):
            tot = tot + xg[POOL_HALO - r:POOL_HALO - r + tt]
        count = jnp.minimum(w, pos + 1).astype(F32)
        diffs.append(tot / count - xg[POOL_HALO:])
    d_ref[0] = _pool_project(diffs, wpool_ref, ps_ref[...]).astype(BF16)


def _convpool_prompt(glu, u, wdw, bdw, lng, lnb, wpool, ps, batch, t, tt):
    g3, u3 = glu.reshape(batch, t, CONV_W), u.reshape(batch, t, POOL_W)
    cur = lambda: pl.BlockSpec((1, tt, CONV_W), lambda b, i: (b, i, 0))
    prev = lambda h: pl.BlockSpec((1, h, CONV_W),
                                  lambda b, i: (b, jnp.maximum(i * (tt // h) - 1, 0), 0))
    c, d = pl.pallas_call(
        _convpool_prompt_kernel, grid=(batch, t // tt),
        in_specs=[cur(), prev(CONV_HALO), cur(), prev(POOL_HALO), _full(wdw.shape), _full(bdw.shape),
                  _full(lng.shape), _full(lnb.shape), _full(wpool.shape), _full(ps.shape)],
        out_specs=[cur(), cur()],
        out_shape=[jax.ShapeDtypeStruct((batch, t, CONV_W), BF16)] * 2,
        compiler_params=_params(2), name="convpool_prompt",
    )(g3, g3, u3, u3, wdw, bdw, lng, lnb, wpool, ps)
    return c.reshape(batch * t, CONV_W), d.reshape(batch * t, POOL_W)


def _convpool_sample_kernel(glu_ref, u_ref, sc_ref, sp_ref, wdw_ref, bdw_ref, lng_ref, lnb_ref,
                            wpool_ref, ps_ref, c_ref, d_ref):
    wdw = wdw_ref[...]
    conv = wdw[CONV_K - 1:CONV_K, :] * glu_ref[...]
    for j in range(CONV_K - 1):
        conv = conv + wdw[j:j + 1, :] * sc_ref[:, j, :]
    c_ref[...] = _conv_tail(conv, bdw_ref[...], lng_ref[...], lnb_ref[...]).astype(BF16)
    u = u_ref[...]
    diffs = []
    for g, w in enumerate(POOL_WINDOWS):
        lanes = slice(g * POOL_GROUP_W, (g + 1) * POOL_GROUP_W)
        tot = u[:, lanes]
        for r in range(1, w):
            tot = tot + sp_ref[:, POOL_CTX - r, lanes]
        diffs.append(tot / float(w) - u[:, lanes])
    d_ref[...] = _pool_project(diffs, wpool_ref, ps_ref[...]).astype(BF16)


def _convpool_sample(glu, u, sc, sp, wdw, bdw, lng, lnb, wpool, ps):
    n = glu.shape[0]
    args = (glu, u, sc, sp, wdw, bdw, lng, lnb, wpool, ps)
    return pl.pallas_call(
        _convpool_sample_kernel, grid=(1,), in_specs=[_full(a.shape) for a in args],
        out_specs=[_full((n, CONV_W)), _full((n, POOL_W))],
        out_shape=[jax.ShapeDtypeStruct((n, CONV_W), BF16), jax.ShapeDtypeStruct((n, POOL_W), BF16)],
        compiler_params=_params(1), name="convpool_sample",
    )(*args)


def kernel(x_prompt, x_sample, cache_k, cache_v, cache_logf, state_conv, state_pool, page_table,
           p_prompt, p_sample, w_att_in, b_forget, w_att_out, w_cp_in, w_dw, b_dw, ln_conv_g,
           ln_conv_b, w_pool, pool_scale, w_cp_out, ln_g, ln_b, w_router, router_bias, w_exp_gate,
           w_exp_up, w_exp_down, w_sh_gate, w_sh_up, w_sh_down, w_ple_proj, w_ple_gate):
    bp, tp, _ = x_prompt.shape
    bs, ts, _ = x_sample.shape
    n_p, n_s = bp * tp, bs * ts
    n_pool = cache_k.shape[1]
    tm_p, tm_s = 256, n_s
    row = lambda a: a.reshape(1, -1)

    xp = x_prompt.reshape(n_p, D_MODEL)
    xs = x_sample.reshape(n_s, D_MODEL)
    outs = {}
    for i in range(DEPTH):
        j = i // 2
        if i % 2 == 0:
            w_in = w_att_in[j]
            wqkv = w_in[:, :3 * ATT_W].astype(BF16)
            wf = w_in[:, 3 * ATT_W:].astype(BF16)
            proj_w = (wqkv, wf, wf.T, row(b_forget[j]), b_forget[j].reshape(-1, 1))
            qp, kp, vp, kpb, vpb, lfp, lfpt = _att_proj(xp, *proj_w, tm_p)
            qs, ks, vs, _, _, lfs, _ = _att_proj(xs, *proj_w, tm_s)
            fq, fk = _fcum(lfp.reshape(bp, tp, H_FOX), lfpt, bp, tp)
            ap, bpj = _prompt_attention(qp, kpb, vpb, fq, fk, bp, tp)
            o_s = _decode_attention(
                qs, ks, vs, lfs,
                cache_k[j].reshape(n_pool, PAGE_SIZE, ATT_W),
                cache_v[j].reshape(n_pool, PAGE_SIZE, ATT_W),
                jnp.swapaxes(cache_logf[j], 1, 2), page_table)
            as_, bs_ = o_s[:, :ATT_W // 2], o_s[:, ATT_W // 2:]
            w_out = w_att_out[j].astype(BF16)
            wa, wb = w_out[:ATT_W // 2], w_out[ATT_W // 2:]
            outs["k_p"] = kp.reshape(1, bp, tp, H_SB + H_FOX, HEAD_DIM)
            outs["v_p"] = vp.reshape(1, bp, tp, H_SB + H_FOX, HEAD_DIM)
            outs["lf_p"] = lfp.reshape(1, bp, tp, H_FOX)
            outs["k_s"] = ks.reshape(1, bs, ts, H_SB + H_FOX, HEAD_DIM)
            outs["v_s"] = vs.reshape(1, bs, ts, H_SB + H_FOX, HEAD_DIM)
            outs["lf_s"] = lfs.reshape(1, bs, ts, H_FOX)
        else:
            w_in = w_cp_in[j].astype(BF16)
            cp_w = (w_dw[j], row(b_dw[j]), row(ln_conv_g[j]), row(ln_conv_b[j]),
                    w_pool[j].astype(BF16), row(pool_scale[j]))
            glu_p, u_p = _cp_proj(xp, w_in, tm_p)
            glu_s, u_s = _cp_proj(xs, w_in, tm_s)
            ap, bpj = _convpool_prompt(glu_p, u_p, *cp_w, bp, tp, 256)
            as_, bs_ = _convpool_sample(glu_s, u_s, state_conv[j], state_pool[j], *cp_w)
            w_out = w_cp_out[j].astype(BF16)
            wa, wb = w_out[:CONV_W], w_out[CONV_W:]
            outs["cv_p"] = glu_p.reshape(bp, tp, CONV_W)[None, :, tp - (CONV_K - 1):]
            outs["pl_p"] = u_p.reshape(bp, tp, POOL_W)[None, :, tp - POOL_CTX:]
            outs["cv_s"] = jnp.concatenate([state_conv[j][:, 1:], glu_s[:, None]], axis=1)[None]
            outs["pl_s"] = jnp.concatenate([state_pool[j][:, 1:], u_s[:, None]], axis=1)[None]
        g0, b0 = row(ln_g[i, 0]), row(ln_b[i, 0])
        xp = _proj_ln(xp, ap, bpj, wa, wb, g0, b0, 512)
        xs = _proj_ln(xs, as_, bs_, wa, wb, g0, b0, tm_s)
        wr = w_router[i].T
        wr_hi = wr.astype(BF16)
        wr_mid = (wr - wr_hi.astype(F32)).astype(BF16)
        layer_w = (wr_hi, wr_mid, router_bias[i].reshape(-1, 1), w_exp_gate[i], w_exp_up[i],
                   w_exp_down[i], w_sh_gate[i].astype(BF16), w_sh_up[i].astype(BF16),
                   w_sh_down[i].astype(BF16), row(ln_g[i, 1]), row(ln_b[i, 1]),
                   w_ple_gate[i].astype(BF16), w_ple_proj[i].astype(BF16))
        xp, xs = _moe_ple([(xp, p_prompt[i].reshape(n_p, PLE_DIM), 512),
                           (xs, p_sample[i].reshape(n_s, PLE_DIM), tm_s)], layer_w)
    return (xp.reshape(bp, tp, D_MODEL), xs.reshape(bs, ts, D_MODEL),
            outs["k_p"], outs["v_p"], outs["lf_p"], outs["k_s"], outs["v_s"], outs["lf_s"],
            outs["cv_p"], outs["cv_s"], outs["pl_p"], outs["pl_s"])
```
